```python
import jax, jax.numpy as jnp
from jax import lax
import numpy as np

D_MODEL = 1024
BATCH = 8
SEQ = 2048
DEPTH = 2
DEC_BATCH = 128
DEC_SEQ = 4
PAST_LEN = 16384
PAGE_SIZE = 128

GLA_HEADS = 4
GLA_DK = D_MODEL // (2 * GLA_HEADS)
GLA_DV = D_MODEL // GLA_HEADS
GLA_RANK = 16
GLA_TAU = 16.0
GLA_CHUNK = 64
SGU_GROUPS = 4
SGU_WIDTH = D_MODEL
SGU_CHUNK = 128
RET_HEADS = 4
RET_DK = D_MODEL // RET_HEADS
RET_DV = 2 * D_MODEL // RET_HEADS
RET_CHUNK = 128
ROPE_BASE = 10000.0
X_HEADS = 4
X_HEAD_DIM = D_MODEL // X_HEADS
MEM_LEN = 256
D_FF = 11 * D_MODEL // 4
N_EXPERTS = 8
TOP_K = 2
N_DENSE = (DEPTH + 1) // 2
N_MOE = DEPTH // 2
EPS = 1e-6
IN_SPLITS = (GLA_HEADS * GLA_DK, GLA_HEADS * GLA_DK, GLA_HEADS * GLA_DV, GLA_HEADS * GLA_DV, GLA_RANK,
             2 * SGU_WIDTH, RET_HEADS * RET_DK, RET_HEADS * RET_DK, RET_HEADS * RET_DV, RET_HEADS * RET_DV,
             3 * D_MODEL)
IN_WIDTH = sum(IN_SPLITS)

kernel_name = 'gla_sgu_retnet_parallel_hybrid_step'

F32 = jnp.float32


def _rms(x):
    xf = x.astype(F32)
    return xf * lax.rsqrt(jnp.mean(xf * xf, axis=-1, keepdims=True) + EPS)


def _rmsnorm(x, g):
    return (_rms(x) * g.astype(F32)).astype(x.dtype)


def _layernorm(x, g, b):
    xf = x.astype(F32)
    mu = jnp.mean(xf, axis=-1, keepdims=True)
    xc = xf - mu
    var = jnp.mean(xc * xc, axis=-1, keepdims=True)
    return (xc * lax.rsqrt(var + EPS) * g.astype(F32) + b.astype(F32)).astype(x.dtype)


def _heads(t, n):
    B, T, W = t.shape
    return t.reshape(B, T, n, W // n).transpose(0, 2, 1, 3)


def _merge_heads(t):
    B, n, T, d = t.shape
    return t.transpose(0, 2, 1, 3).reshape(B, T, n * d)


def _chunk_len(T, c):
    return c if T % c == 0 else T


def _to_chunks(t, c):
    B, H, T, d = t.shape
    return jnp.moveaxis(t.astype(F32).reshape(B, H, T // c, c, d), 2, 0)


def _from_chunks(t):
    n, B, H, c, d = t.shape
    return jnp.moveaxis(t, 0, 2).reshape(B, H, n * c, d)


def _rotary(x, pos):
    half = x.shape[-1] // 2
    inv = 1.0 / (ROPE_BASE ** (jnp.arange(half, dtype=F32) / half))
    ang = pos.astype(F32)[:, None] * inv[None, :]
    cos, sin = jnp.cos(ang), jnp.sin(ang)
    xf = x.astype(F32)
    x1, x2 = xf[..., :half], xf[..., half:]
    return jnp.concatenate([x1 * cos - x2 * sin, x2 * cos + x1 * sin], axis=-1).astype(x.dtype)


def _gla_chunked(q, k, v, log_a, s0, chunk):
    mask = jnp.tril(jnp.ones((chunk, chunk), dtype=bool))[:, :, None]

    def step(s, inp):
        qi, ki, vi, ai = inp
        b = jnp.cumsum(ai, axis=2)
        inter = jnp.einsum('bhcd,bhde->bhce', qi * jnp.exp(b), s)
        diff = b[:, :, :, None, :] - b[:, :, None, :, :]
        decay = jnp.exp(jnp.where(mask, diff, -jnp.inf))
        scores = jnp.einsum('bhid,bhjd,bhijd->bhij', qi, ki, decay)
        intra = jnp.einsum('bhij,bhje->bhie', scores, vi)
        b_end = b[:, :, -1:, :]
        s_new = jnp.exp(b_end)[:, :, 0, :, None] * s + jnp.einsum('bhcd,bhce->bhde', ki * jnp.exp(b_end - b), vi)
        return s_new, inter + intra

    s_fin, o = lax.scan(step, s0.astype(F32),
                        (_to_chunks(q, chunk), _to_chunks(k, chunk), _to_chunks(v, chunk), _to_chunks(log_a, chunk)))
    return _from_chunks(o), s_fin


def _retention_chunked(q, k, v, s0, chunk):
    H = q.shape[1]
    log_gamma = jnp.log1p(-jnp.exp2(-5.0 - jnp.arange(H, dtype=F32)))
    idx = jnp.arange(chunk, dtype=F32)
    rel = idx[:, None] - idx[None, :]
    dmat = jnp.where(rel >= 0, jnp.exp(log_gamma[:, None, None] * jnp.maximum(rel, 0.0)), 0.0)
    q_dec = jnp.exp(log_gamma[:, None] * (idx + 1.0))[None, :, :, None]
    k_dec = jnp.exp(log_gamma[:, None] * (chunk - 1.0 - idx))[None, :, :, None]
    c_dec = jnp.exp(log_gamma * chunk)[None, :, None, None]

    def step(s, inp):
        qi, ki, vi = inp
        inter = jnp.einsum('bhcd,bhde->bhce', qi, s) * q_dec
        scores = jnp.einsum('bhid,bhjd->bhij', qi, ki) * dmat
        intra = jnp.einsum('bhij,bhje->bhie', scores, vi)
        s_new = c_dec * s + jnp.einsum('bhcd,bhce->bhde', ki * k_dec, vi)
        return s_new, inter + intra

    s_fin, o = lax.scan(step, s0.astype(F32),
                        (_to_chunks(q, chunk), _to_chunks(k, chunk), _to_chunks(v, chunk)))
    return _from_chunks(o), s_fin


def _sgu(u, vn, ws, bs):
    B, T, W = vn.shape
    c = min(SGU_CHUNK, T)
    wm = jnp.tril(ws[:, :c, :c])
    vc = vn.reshape(B, T // c, c, SGU_GROUPS, W // SGU_GROUPS)
    mixed = jnp.einsum('gts,bnsgd->bntgd', wm, vc) + bs[:, :c].T[None, None, :, :, None]
    return u * mixed.reshape(B, T, W)


def _mixer(h, pos, s_gla, s_ret, l, p):
    B, T, _ = h.shape
    cols = h @ p['w_in'][l]
    gq, gk, gv, gr, ga, suv, rq, rk, rv, rg, mg = jnp.split(cols, np.cumsum(IN_SPLITS)[:-1].tolist(), axis=-1)
    log_a = jax.nn.log_sigmoid((ga @ p['gla_wa2'][l] + p['gla_ba'][l]).astype(F32)) / GLA_TAU
    o_a, s_gla_new = _gla_chunked(_heads(gq, GLA_HEADS) * (GLA_DK ** -0.5), _heads(gk, GLA_HEADS),
                                  _heads(gv, GLA_HEADS), _heads(log_a, GLA_HEADS), s_gla, _chunk_len(T, GLA_CHUNK))
    o_a = _merge_heads(_rmsnorm(o_a, p['gla_norm'][l]).astype(h.dtype)) * jax.nn.silu(gr)
    u, v = jnp.split(jax.nn.gelu(suv), 2, axis=-1)
    vn = _layernorm(v, p['sgu_ln_g'][l], p['sgu_ln_b'][l])
    o_b = _sgu(u, vn, p['sgu_ws'][l], p['sgu_bs'][l])
    q_r = _rotary(_heads(rq, RET_HEADS), pos)
    k_r = _rotary(_heads(rk, RET_HEADS), pos) * (RET_DK ** -0.5)
    o_c, s_ret_new = _retention_chunked(q_r, k_r, _heads(rv, RET_HEADS), s_ret, _chunk_len(T, RET_CHUNK))
    o_c = _merge_heads(_rms(o_c).astype(h.dtype)) * jax.nn.silu(rg)
    g_a, g_b, g_c = jnp.split(jax.nn.sigmoid(mg), 3, axis=-1)
    y = g_a * (o_a @ p['proj_a'][l]) + g_b * (o_b @ p['proj_b'][l]) + g_c * (o_c @ p['proj_c'][l])
    return y @ p['w_out'][l], s_gla_new, s_ret_new, vn


def _mem_kv(mem, g, wk, wv):
    B, M, _ = mem.shape
    m = _rmsnorm(mem, g)
    return (m @ wk).reshape(B, M, X_HEADS, X_HEAD_DIM), (m @ wv).reshape(B, M, X_HEADS, X_HEAD_DIM)


def _cross_attn(h, mk, mv, wq, wo):
    B, T, _ = h.shape
    q = (h @ wq).reshape(B, T, X_HEADS, X_HEAD_DIM)
    s = jnp.einsum('bthd,bmhd->bhtm', q, mk).astype(F32) * (X_HEAD_DIM ** -0.5)
    a = jax.nn.softmax(s, axis=-1).astype(h.dtype)
    o = jnp.einsum('bhtm,bmhd->bthd', a, mv).reshape(B, T, X_HEADS * X_HEAD_DIM)
    return o @ wo


def _swiglu(h, w1, w3, w2):
    return (jax.nn.silu(h @ w1) * (h @ w3)) @ w2


def _moe(h, router, w1, w3, w2):
    probs = jax.nn.softmax((h @ router).astype(F32), axis=-1)
    top_p, top_i = lax.top_k(probs, TOP_K)
    top_p = top_p / jnp.sum(top_p, axis=-1, keepdims=True)
    gates = jnp.sum(jax.nn.one_hot(top_i, N_EXPERTS, dtype=F32) * top_p[..., None], axis=-2).astype(h.dtype)
    out = jnp.zeros_like(h)
    for e in range(N_EXPERTS):
        out = out + gates[..., e:e + 1] * _swiglu(h, w1[e], w3[e], w2[e])
    return out


def _trunk(x, pos, s_gla, s_ret, mem_k, mem_v, p):
    new_gla, new_ret, new_v = [], [], []
    for l in range(DEPTH):
        mix, sg, sr, vn = _mixer(_rmsnorm(x, p['norm_mix'][l]), pos, s_gla[l], s_ret[l], l, p)
        x = x + mix
        x = x + _cross_attn(_rmsnorm(x, p['norm_x'][l]), mem_k[l], mem_v[l], p['x_wq'][l], p['x_wo'][l])
        h = _rmsnorm(x, p['norm_ffn'][l])
        if l % 2 == 0:
            x = x + _swiglu(h, p['ffn_w1'][l // 2], p['ffn_w3'][l // 2], p['ffn_w2'][l // 2])
        else:
            x = x + _moe(h, p['moe_router'][l // 2], p['moe_w1'][l // 2], p['moe_w3'][l // 2], p['moe_w2'][l // 2])
        new_gla.append(sg)
        new_ret.append(sr)
        new_v.append(vn)
    return _rmsnorm(x, p['norm_final']), jnp.stack(new_gla), jnp.stack(new_ret), new_v


def setup_inputs(seed: int = 0) -> dict:
    key = jax.random.key(seed)
    ks = iter(jax.random.split(key, 48))

    def nrm(shape, scale):
        return jax.random.normal(next(ks), shape, F32) * scale

    def gain(shape):
        return 1.0 + nrm(shape, 0.02)

    return {
        'x_prompt': nrm((BATCH, SEQ, D_MODEL), 1.0),
        'x_sample': nrm((DEC_BATCH, DEC_SEQ, D_MODEL), 1.0),
        'state_gla': nrm((DEPTH, DEC_BATCH, GLA_HEADS, GLA_DK, GLA_DV), 0.5),
        'state_ret': nrm((DEPTH, DEC_BATCH, RET_HEADS, RET_DK, RET_DV), 0.5),
        'cache_mem_k': nrm((DEPTH, DEC_BATCH, MEM_LEN, X_HEADS, X_HEAD_DIM), 1.0),
        'cache_mem_v': nrm((DEPTH, DEC_BATCH, MEM_LEN, X_HEADS, X_HEAD_DIM), 1.0),
        'mem_prompt': nrm((BATCH, MEM_LEN, D_MODEL), 1.0),
        'norm_mix': gain((DEPTH, D_MODEL)),
        'norm_x': gain((DEPTH, D_MODEL)),
        'norm_mem': gain((DEPTH, D_MODEL)),
        'norm_ffn': gain((DEPTH, D_MODEL)),
        'norm_final': gain((D_MODEL,)),
        'w_in': nrm((DEPTH, D_MODEL, IN_WIDTH), D_MODEL ** -0.5),
        'gla_wa2': nrm((DEPTH, GLA_RANK, GLA_HEADS * GLA_DK), GLA_RANK ** -0.5),
        'gla_ba': nrm((DEPTH, GLA_HEADS * GLA_DK), 0.1),
        'gla_norm': gain((DEPTH, GLA_DV)),
        'sgu_ln_g': gain((DEPTH, SGU_WIDTH)),
        'sgu_ln_b': nrm((DEPTH, SGU_WIDTH), 0.02),
        'sgu_ws': nrm((DEPTH, SGU_GROUPS, SGU_CHUNK, SGU_CHUNK), SGU_CHUNK ** -0.5),
        'sgu_bs': 1.0 + nrm((DEPTH, SGU_GROUPS, SGU_CHUNK), 0.1),
        'proj_a': nrm((DEPTH, GLA_HEADS * GLA_DV, D_MODEL), (GLA_HEADS * GLA_DV) ** -0.5),
        'proj_b': nrm((DEPTH, SGU_WIDTH, D_MODEL), SGU_WIDTH ** -0.5),
        'proj_c': nrm((DEPTH, RET_HEADS * RET_DV, D_MODEL), (RET_HEADS * RET_DV) ** -0.5),
        'w_out': nrm((DEPTH, D_MODEL, D_MODEL), D_MODEL ** -0.5),
        'x_wq': nrm((DEPTH, D_MODEL, X_HEADS * X_HEAD_DIM), D_MODEL ** -0.5),
        'x_wk': nrm((DEPTH, D_MODEL, X_HEADS * X_HEAD_DIM), D_MODEL ** -0.5),
        'x_wv': nrm((DEPTH, D_MODEL, X_HEADS * X_HEAD_DIM), D_MODEL ** -0.5),
        'x_wo': nrm((DEPTH, X_HEADS * X_HEAD_DIM, D_MODEL), (X_HEADS * X_HEAD_DIM) ** -0.5),
        'ffn_w1': nrm((N_DENSE, D_MODEL, D_FF), D_MODEL ** -0.5),
        'ffn_w3': nrm((N_DENSE, D_MODEL, D_FF), D_MODEL ** -0.5),
        'ffn_w2': nrm((N_DENSE, D_FF, D_MODEL), D_FF ** -0.5),
        'moe_router': nrm((N_MOE, D_MODEL, N_EXPERTS), D_MODEL ** -0.5),
        'moe_w1': nrm((N_MOE, N_EXPERTS, D_MODEL, D_FF), D_MODEL ** -0.5),
        'moe_w3': nrm((N_MOE, N_EXPERTS, D_MODEL, D_FF), D_MODEL ** -0.5),
        'moe_w2': nrm((N_MOE, N_EXPERTS, D_FF, D_MODEL), D_FF ** -0.5),
    }


def reference(x_prompt, x_sample, state_gla, state_ret, cache_mem_k, cache_mem_v, mem_prompt,
              norm_mix, norm_x, norm_mem, norm_ffn, norm_final, w_in, gla_wa2, gla_ba, gla_norm,
              sgu_ln_g, sgu_ln_b, sgu_ws, sgu_bs, proj_a, proj_b, proj_c, w_out,
              x_wq, x_wk, x_wv, x_wo, ffn_w1, ffn_w3, ffn_w2, moe_router, moe_w1, moe_w3, moe_w2):
    p = dict(norm_mix=norm_mix, norm_x=norm_x, norm_ffn=norm_ffn, norm_final=norm_final, w_in=w_in,
             gla_wa2=gla_wa2, gla_ba=gla_ba, gla_norm=gla_norm, sgu_ln_g=sgu_ln_g, sgu_ln_b=sgu_ln_b,
             sgu_ws=sgu_ws, sgu_bs=sgu_bs, proj_a=proj_a, proj_b=proj_b, proj_c=proj_c, w_out=w_out,
             x_wq=x_wq, x_wo=x_wo, ffn_w1=ffn_w1, ffn_w3=ffn_w3, ffn_w2=ffn_w2,
             moe_router=moe_router, moe_w1=moe_w1, moe_w3=moe_w3, moe_w2=moe_w2)
    dt = x_prompt.dtype
    Bp, Tp, _ = x_prompt.shape
    Bs, Ts, _ = x_sample.shape

    kv = [_mem_kv(mem_prompt, norm_mem[l], x_wk[l], x_wv[l]) for l in range(DEPTH)]
    p_mem_k = jnp.stack([kv_l[0] for kv_l in kv])
    p_mem_v = jnp.stack([kv_l[1] for kv_l in kv])
    zeros_gla = jnp.zeros((DEPTH, Bp, GLA_HEADS, GLA_DK, GLA_DV), F32)
    zeros_ret = jnp.zeros((DEPTH, Bp, RET_HEADS, RET_DK, RET_DV), F32)
    pos_p = jnp.arange(Tp, dtype=jnp.int32)
    y_prompt, p_gla, p_ret, _ = _trunk(x_prompt, pos_p, zeros_gla, zeros_ret, p_mem_k, p_mem_v, p)

    pos_s = PAST_LEN + jnp.arange(Ts, dtype=jnp.int32)
    y_sample, s_gla, s_ret, s_v = _trunk(x_sample, pos_s, state_gla, state_ret, cache_mem_k, cache_mem_v, p)
    s_sgu_v = jnp.stack(s_v)

    return (y_prompt, y_sample, p_gla.astype(dt), p_ret.astype(dt), p_mem_k, p_mem_v,
            s_gla.astype(dt), s_ret.astype(dt), s_sgu_v)
```

```python
import functools

import numpy as np
import jax
import jax.numpy as jnp
from jax import lax
from jax.experimental import pallas as pl
from jax.experimental.pallas import tpu as pltpu

F32 = jnp.float32
BF16 = jnp.bfloat16

D_MODEL = 1024
DEPTH = 2
PAST_LEN = 16384
GLA_HEADS = 4
GLA_DK = 128
GLA_DV = 256
GLA_RANK = 16
GLA_TAU = 16.0
SGU_GROUPS = 4
SGU_CHUNK = 128
RET_HEADS = 4
RET_DK = 256
RET_DV = 512
ROPE_BASE = 10000.0
X_HEADS = 4
X_HEAD_DIM = 256
D_FF = 2816
N_EXPERTS = 8
EPS = 1e-6

LANES = 128
SAMPLE_PAD_T = 16
GLA_CHUNK = 64
RET_CHUNK = 128
MIB = 2 ** 20

_O_GQ, _O_GK, _O_GV, _O_GR, _O_GA, _O_SUV, _O_RQ, _O_RK, _O_RV, _O_RG, _O_MG, _O_END = (
    0, 512, 1024, 2048, 3072, 3088, 5136, 6160, 7184, 9232, 11280, 14352)
PROJ_W = _O_END - GLA_RANK
_C_GQ, _C_GK, _C_GV, _C_GR, _C_SU, _C_SV, _C_RQ, _C_RK, _C_MG, _C_RV, _C_RG = (
    0, 512, 1024, 2048, 3072, 4096, 5120, 6144, 7168, 10240, 12288)


def _cparams(sem, vmem_mib):
    return pltpu.CompilerParams(dimension_semantics=sem, vmem_limit_bytes=vmem_mib * MIB)


def _dot(a, b):
    return jnp.dot(a, b, preferred_element_type=F32)


def _dot_nt(a, b):
    return lax.dot_general(a, b, (((1,), (1,)), ((), ())), preferred_element_type=F32)


def _dot_tn(a, b):
    return lax.dot_general(a, b, (((0,), (0,)), ((), ())), preferred_element_type=F32)


def _sigmoid(x):
    return 1.0 / (1.0 + jnp.exp(-x))


def _silu(x):
    return x * _sigmoid(x)


def _gelu_tanh(x):
    return 0.5 * x * (1.0 + jnp.tanh(0.7978845608028654 * (x + 0.044715 * (x * x * x))))


def _rms_rows(x):
    return x * lax.rsqrt(jnp.mean(x * x, axis=-1, keepdims=True) + EPS)


def _norm_matmul_body(x_ref, g_ref, w_ref, o_ref, h_ref):
    @pl.when(pl.program_id(1) == 0)
    def _():
        h_ref[...] = (_rms_rows(x_ref[...]) * g_ref[...]).astype(BF16)

    o_ref[...] = _dot(h_ref[...], w_ref[...]).astype(o_ref.dtype)


def _norm_matmul(x, g, w, *, tm, tn, out_dtype):
    n, d = x.shape
    nout = w.shape[1]
    return pl.pallas_call(
        _norm_matmul_body,
        grid=(n // tm, nout // tn),
        in_specs=[pl.BlockSpec((tm, d), lambda i, j: (i, 0)),
                  pl.BlockSpec((1, d), lambda i, j: (0, 0)),
                  pl.BlockSpec((d, tn), lambda i, j: (0, j))],
        out_specs=pl.BlockSpec((tm, tn), lambda i, j: (i, j)),
        out_shape=jax.ShapeDtypeStruct((n, nout), out_dtype),
        scratch_shapes=[pltpu.VMEM((tm, d), BF16)],
        compiler_params=_cparams(("parallel", "arbitrary"), 56),
    )(x, g.reshape(1, d), w)


def _in_proj_body(x_ref, g_ref, w_ref, wa_ref, o_ref, a_ref, h_ref):
    @pl.when(pl.program_id(1) == 0)
    def _():
        h = (_rms_rows(x_ref[...]) * g_ref[...]).astype(BF16)
        h_ref[...] = h
        a_ref[...] = _dot(h, wa_ref[...]).astype(a_ref.dtype)

    o_ref[...] = _dot(h_ref[...], w_ref[...]).astype(o_ref.dtype)


def _in_proj(x, g, w, wa, *, tm, tn):
    n, d = x.shape
    nout = w.shape[1]
    return pl.pallas_call(
        _in_proj_body,
        grid=(n // tm, nout // tn),
        in_specs=[pl.BlockSpec((tm, d), lambda i, j: (i, 0)),
                  pl.BlockSpec((1, d), lambda i, j: (0, 0)),
                  pl.BlockSpec((d, tn), lambda i, j: (0, j)),
                  pl.BlockSpec((d, LANES), lambda i, j: (0, 0))],
        out_specs=[pl.BlockSpec((tm, tn), lambda i, j: (i, j)),
                   pl.BlockSpec((tm, LANES), lambda i, j: (i, 0))],
        out_shape=[jax.ShapeDtypeStruct((n, nout), BF16),
                   jax.ShapeDtypeStruct((n, LANES), BF16)],
        scratch_shapes=[pltpu.VMEM((tm, d), BF16)],
        compiler_params=_cparams(("parallel", "arbitrary"), 56),
    )(x, g.reshape(1, d), w, wa)


def _gla_tables(c, t_valid):
    nl = int(np.log2(c))
    assert 1 << nl == c
    idx = np.arange(c)
    tril = (idx[:, None] >= idx[None, :]) & (idx[None, :] < t_valid)
    cum = [tril]
    masks = []
    for l in range(nl):
        s = c >> (l + 1)
        ref = (idx // (2 * s)) * (2 * s) + s - 1
        cum.append(tril[ref])
        same = (idx[:, None] // (2 * s)) == (idx[None, :] // (2 * s))
        masks.append(same & ((idx[:, None] % (2 * s)) >= s) & ((idx[None, :] % (2 * s)) < s))
    masks.append(idx[:, None] == idx[None, :])
    return (jnp.asarray(np.concatenate(cum, 0), BF16), jnp.asarray(np.stack(masks), F32), nl)


def _gla_body(q_ref, k_ref, v_ref, r_ref, a_ref, wa_ref, ba_ref, gn_ref, cum_ref, mask_ref, s0_ref,
              o_ref, sn_ref, s_ref, *, chunk, nl, n_chunks):
    t = pl.program_id(1)

    @pl.when(t == 0)
    def _():
        s_ref[...] = s0_ref[0]

    c = chunk

    def chunk_step(ci, carry):
        r0 = pl.multiple_of(ci * c, c)
        rows = pl.ds(r0, c)
        ga = a_ref[0, rows, :]
        for h in range(GLA_HEADS):
            kcols = slice(h * GLA_DK, (h + 1) * GLA_DK)
            vcols = slice(h * GLA_DV, (h + 1) * GLA_DV)
            x = _dot(ga, wa_ref[h]) + ba_ref[h]
            la = (jnp.minimum(x, 0.0) - jnp.log(1.0 + jnp.exp(-jnp.abs(x)))) * (1.0 / GLA_TAU)
            la_hi = la.astype(BF16)
            la_lo = (la - la_hi.astype(F32)).astype(BF16)
            cs = _dot(cum_ref[...], jnp.concatenate([la_hi, la_lo], axis=1))
            cs = cs[:, :GLA_DK] + cs[:, GLA_DK:]
            b = cs[:c]
            q = q_ref[0, rows, kcols].astype(F32) * (GLA_DK ** -0.5)
            k = k_ref[0, rows, kcols].astype(F32)
            v = v_ref[0, rows, vcols]
            s = s_ref[h]
            inter = _dot((q * jnp.exp(b)).astype(BF16), s.astype(BF16))
            scores = mask_ref[nl] * _dot_nt(q.astype(BF16), k.astype(BF16))
            for l in range(nl):
                f = jnp.exp(-jnp.abs(b - cs[(l + 1) * c:(l + 2) * c]))
                scores = scores + mask_ref[l] * _dot_nt((q * f).astype(BF16), (k * f).astype(BF16))
            o = inter + _dot(scores.astype(BF16), v)
            b_end = b[c - 1:c, :]
            kd = (k * jnp.exp(b_end - b)).astype(BF16)
            e_col = jnp.broadcast_to(jnp.exp(b_end), (GLA_DK, GLA_DK)).T
            s_ref[h] = jnp.concatenate([e_col, e_col], axis=1) * s + _dot_tn(kd, v)
            on = _rms_rows(o) * gn_ref[...]
            o_ref[0, rows, vcols] = (on * _silu(r_ref[0, rows, vcols].astype(F32))).astype(o_ref.dtype)
        return carry

    lax.fori_loop(0, n_chunks, chunk_step, 0)

    @pl.when(t == pl.num_programs(1) - 1)
    def _():
        sn_ref[0] = s_ref[...]


def _gla(cols, ga, wa2, ba, gnorm, s0, *, tb, chunk, t_valid):
    bsz, tlen, _ = cols.shape
    cum, masks, nl = _gla_tables(chunk, t_valid)
    kq, kv = GLA_HEADS * GLA_DK, GLA_HEADS * GLA_DV
    body = functools.partial(_gla_body, chunk=chunk, nl=nl, n_chunks=tb // chunk)
    const3 = lambda b, t: (0, 0, 0)
    return pl.pallas_call(
        body,
        grid=(bsz, tlen // tb),
        in_specs=[pl.BlockSpec((1, tb, kq), lambda b, t: (b, t, _C_GQ // kq)),
                  pl.BlockSpec((1, tb, kq), lambda b, t: (b, t, _C_GK // kq)),
                  pl.BlockSpec((1, tb, kv), lambda b, t: (b, t, _C_GV // kv)),
                  pl.BlockSpec((1, tb, kv), lambda b, t: (b, t, _C_GR // kv)),
                  pl.BlockSpec((1, tb, LANES), lambda b, t: (b, t, 0)),
                  pl.BlockSpec((GLA_HEADS, LANES, GLA_DK), const3),
                  pl.BlockSpec((GLA_HEADS, 1, GLA_DK), const3),
                  pl.BlockSpec((1, GLA_DV), lambda b, t: (0, 0)),
                  pl.BlockSpec(cum.shape, lambda b, t: (0, 0)),
                  pl.BlockSpec(masks.shape, const3),
                  pl.BlockSpec((1, GLA_HEADS, GLA_DK, GLA_DV), lambda b, t: (b, 0, 0, 0))],
        out_specs=[pl.BlockSpec((1, tb, kv), lambda b, t: (b, t, 0)),
                   pl.BlockSpec((1, GLA_HEADS, GLA_DK, GLA_DV), lambda b, t: (b, 0, 0, 0))],
        out_shape=[jax.ShapeDtypeStruct((bsz, tlen, kv), BF16),
                   jax.ShapeDtypeStruct(s0.shape, F32)],
        scratch_shapes=[pltpu.VMEM((GLA_HEADS, GLA_DK, GLA_DV), F32)],
        compiler_params=_cparams(("parallel", "arbitrary"), 32),
    )(cols, cols, cols, cols, ga, wa2, ba, gnorm, cum, masks, s0)


def _ret_tables(c, t_valid):
    log_gamma = jnp.log1p(-jnp.exp2(-5.0 - jnp.arange(RET_HEADS, dtype=F32)))
    idx = jnp.arange(c, dtype=F32)
    rel = idx[:, None] - idx[None, :]
    dmat = jnp.where(rel >= 0, jnp.exp(log_gamma[:, None, None] * jnp.maximum(rel, 0.0)), 0.0)
    q_dec = jnp.exp(log_gamma[:, None] * (idx + 1.0))
    k_dec = jnp.exp(log_gamma[:, None] * jnp.maximum(t_valid - 1.0 - idx, 0.0))
    c_dec = jnp.exp(log_gamma * t_valid)
    return (dmat,
            jnp.broadcast_to(q_dec[:, :, None], (RET_HEADS, c, RET_DV)),
            jnp.broadcast_to(k_dec[:, :, None], (RET_HEADS, c, RET_DK)),
            jnp.broadcast_to(c_dec[:, None, None], (RET_HEADS, 1, RET_DV)))


def _rope_tables(pos):
    half = RET_DK // 2
    inv = 1.0 / (ROPE_BASE ** (jnp.arange(half, dtype=F32) / half))
    ang = pos.astype(F32)[:, None] * inv[None, :]
    return jnp.cos(ang), jnp.sin(ang)


def _ret_body(q_ref, k_ref, v_ref, g_ref, cos_ref, sin_ref, dmat_ref, qd_ref, kd_ref, cd_ref, s0_ref,
              o_ref, sn_ref, s_ref, *, chunk, n_chunks):
    t = pl.program_id(1)

    @pl.when(t == 0)
    def _():
        s_ref[...] = s0_ref[0]

    c = chunk
    half = RET_DK // 2

    def rope(x, cos, sin):
        x1, x2 = x[:, :half], x[:, half:]
        return jnp.concatenate([x1 * cos - x2 * sin, x2 * cos + x1 * sin], axis=1)

    def chunk_step(ci, carry):
        r0 = pl.multiple_of(ci * c, c)
        rows = pl.ds(r0, c)
        cos = cos_ref[rows, :]
        sin = sin_ref[rows, :]
        for h in range(RET_HEADS):
            kcols = slice(h * RET_DK, (h + 1) * RET_DK)
            vcols = slice(h * RET_DV, (h + 1) * RET_DV)
            q = rope(q_ref[0, rows, kcols].astype(F32), cos, sin)
            k = rope(k_ref[0, rows, kcols].astype(F32), cos, sin) * (RET_DK ** -0.5)
            v = v_ref[0, rows, vcols]
            s = s_ref[h]
            inter = _dot(q.astype(BF16), s.astype(BF16)) * qd_ref[h]
            scores = _dot_nt(q.astype(BF16), k.astype(BF16)) * dmat_ref[h]
            o = inter + _dot(scores.astype(BF16), v)
            s_ref[h] = cd_ref[h] * s + _dot_tn((k * kd_ref[h]).astype(BF16), v)
            o_ref[0, rows, vcols] = (_rms_rows(o) * _silu(g_ref[0, rows, vcols].astype(F32))).astype(o_ref.dtype)
        return carry

    lax.fori_loop(0, n_chunks, chunk_step, 0)

    @pl.when(t == pl.num_programs(1) - 1)
    def _():
        sn_ref[0] = s_ref[...]


def _ret(cols, pos, s0, *, tb, chunk, t_valid):
    bsz, tlen, _ = cols.shape
    dmat, q_dec, k_dec, c_dec = _ret_tables(chunk, t_valid)
    cos, sin = _rope_tables(pos)
    kq, kv = RET_HEADS * RET_DK, RET_HEADS * RET_DV
    body = functools.partial(_ret_body, chunk=chunk, n_chunks=tb // chunk)
    const3 = lambda b, t: (0, 0, 0)
    return pl.pallas_call(
        body,
        grid=(bsz, tlen // tb),
        in_specs=[pl.BlockSpec((1, tb, kq), lambda b, t: (b, t, _C_RQ // kq)),
                  pl.BlockSpec((1, tb, kq), lambda b, t: (b, t, _C_RK // kq)),
                  pl.BlockSpec((1, tb, kv), lambda b, t: (b, t, _C_RV // kv)),
                  pl.BlockSpec((1, tb, kv), lambda b, t: (b, t, _C_RG // kv)),
                  pl.BlockSpec((tb, RET_DK // 2), lambda b, t: (t, 0)),
                  pl.BlockSpec((tb, RET_DK // 2), lambda b, t: (t, 0)),
                  pl.BlockSpec(dmat.shape, const3),
                  pl.BlockSpec(q_dec.shape, const3),
                  pl.BlockSpec(k_dec.shape, const3),
                  pl.BlockSpec(c_dec.shape, const3),
                  pl.BlockSpec((1, RET_HEADS, RET_DK, RET_DV), lambda b, t: (b, 0, 0, 0))],
        out_specs=[pl.BlockSpec((1, tb, kv), lambda b, t: (b, t, 0)),
                   pl.BlockSpec((1, RET_HEADS, RET_DK, RET_DV), lambda b, t: (b, 0, 0, 0))],
        out_shape=[jax.ShapeDtypeStruct((bsz, tlen, kv), BF16),
                   jax.ShapeDtypeStruct(s0.shape, F32)],
        scratch_shapes=[pltpu.VMEM((RET_HEADS, RET_DK, RET_DV), F32)],
        compiler_params=_cparams(("parallel", "arbitrary"), 40),
    )(cols, cols, cols, cols, cos, sin, dmat, q_dec, k_dec, c_dec, s0)


def _sgu_body(u_ref, v_ref, ws_ref, bs_ref, lg_ref, lb_ref, o_ref, *maybe_vn_ref, chunk, n_chunks):
    c = chunk
    gw = D_MODEL // SGU_GROUPS
    ri = lax.broadcasted_iota(jnp.int32, (c, c), 0)
    cj = lax.broadcasted_iota(jnp.int32, (c, c), 1)
    causal = ri >= cj
    for ci in range(n_chunks):
        rows = slice(ci * c, (ci + 1) * c)
        u = _gelu_tanh(u_ref[0, rows, :].astype(F32))
        v = _gelu_tanh(v_ref[0, rows, :].astype(F32))
        xc = v - jnp.mean(v, axis=-1, keepdims=True)
        vn = xc * lax.rsqrt(jnp.mean(xc * xc, axis=-1, keepdims=True) + EPS) * lg_ref[...] + lb_ref[...]
        if maybe_vn_ref:
            maybe_vn_ref[0][0, rows, :] = vn
        for g in range(SGU_GROUPS):
            cols = slice(g * gw, (g + 1) * gw)
            wm = jnp.where(causal, ws_ref[g], 0.0).astype(BF16)
            mixed = _dot(wm, vn[:, cols].astype(BF16)) + bs_ref[g]
            o_ref[0, rows, cols] = (u[:, cols] * mixed).astype(o_ref.dtype)


def _sgu(cols, ws, bs, ln_g, ln_b, *, tb, chunk, emit_vn):
    bsz, tlen, _ = cols.shape
    gw = D_MODEL // SGU_GROUPS
    bs_b = jnp.broadcast_to(bs[:, :, None], (SGU_GROUPS, chunk, gw))
    body = functools.partial(_sgu_body, chunk=chunk, n_chunks=tb // chunk)
    out_specs = [pl.BlockSpec((1, tb, D_MODEL), lambda b, t: (b, t, 0))]
    out_shape = [jax.ShapeDtypeStruct((bsz, tlen, D_MODEL), BF16)]
    if emit_vn:
        out_specs.append(pl.BlockSpec((1, tb, D_MODEL), lambda b, t: (b, t, 0)))
        out_shape.append(jax.ShapeDtypeStruct((bsz, tlen, D_MODEL), F32))
    return pl.pallas_call(
        body,
        grid=(bsz, tlen // tb),
        in_specs=[pl.BlockSpec((1, tb, D_MODEL), lambda b, t: (b, t, _C_SU // D_MODEL)),
                  pl.BlockSpec((1, tb, D_MODEL), lambda b, t: (b, t, _C_SV // D_MODEL)),
                  pl.BlockSpec(ws.shape, lambda b, t: (0, 0, 0)),
                  pl.BlockSpec(bs_b.shape, lambda b, t: (0, 0, 0)),
                  pl.BlockSpec((1, D_MODEL), lambda b, t: (0, 0)),
                  pl.BlockSpec((1, D_MODEL), lambda b, t: (0, 0))],
        out_specs=out_specs,
        out_shape=out_shape,
        compiler_params=_cparams(("parallel", "parallel"), 32),
    )(cols, cols, ws, bs_b, ln_g.reshape(1, -1), ln_b.reshape(1, -1))


def _merge_body(x_ref, oa_ref, ob_ref, oc_ref, ga_ref, gb_ref, gc_ref, pa_ref, pb_ref, pc_ref, wo_ref, o_ref):
    y = _sigmoid(ga_ref[...].astype(F32)) * _dot(oa_ref[...], pa_ref[...])
    y = y + _sigmoid(gb_ref[...].astype(F32)) * _dot(ob_ref[...], pb_ref[...])
    y = y + _sigmoid(gc_ref[...].astype(F32)) * _dot(oc_ref[...], pc_ref[...])
    o_ref[...] = x_ref[...] + _dot(y.astype(BF16), wo_ref[...])


def _merge(x, oa, ob, oc, cols, pa, pb, pc, wo, *, tm):
    n, d = x.shape
    row = lambda i: (i, 0)
    const = lambda i: (0, 0)
    mg0 = _C_MG // d
    return pl.pallas_call(
        _merge_body,
        grid=(n // tm,),
        in_specs=[pl.BlockSpec((tm, d), row),
                  pl.BlockSpec((tm, oa.shape[1]), row),
                  pl.BlockSpec((tm, ob.shape[1]), row),
                  pl.BlockSpec((tm, oc.shape[1]), row),
                  pl.BlockSpec((tm, d), lambda i: (i, mg0)),
                  pl.BlockSpec((tm, d), lambda i: (i, mg0 + 1)),
                  pl.BlockSpec((tm, d), lambda i: (i, mg0 + 2)),
                  pl.BlockSpec(pa.shape, const),
                  pl.BlockSpec(pb.shape, const),
                  pl.BlockSpec(pc.shape, const),
                  pl.BlockSpec(wo.shape, const)],
        out_specs=pl.BlockSpec((tm, d), row),
        out_shape=jax.ShapeDtypeStruct((n, d), F32),
        compiler_params=_cparams(("parallel",), 56),
    )(x, oa, ob, oc, cols, cols, cols, pa, pb, pc, wo)


def _xattn_body(q_ref, k_ref, v_ref, o_ref):
    q = q_ref[0]
    for h in range(X_HEADS):
        cols = slice(h * X_HEAD_DIM, (h + 1) * X_HEAD_DIM)
        s = _dot_nt(q[:, cols], k_ref[0, :, cols].astype(BF16)) * (X_HEAD_DIM ** -0.5)
        e = jnp.exp(s - jnp.max(s, axis=-1, keepdims=True))
        p = e / jnp.sum(e, axis=-1, keepdims=True)
        o_ref[0, :, cols] = _dot(p.astype(BF16), v_ref[0, :, cols].astype(BF16)).astype(o_ref.dtype)


def _xattn(q, mk, mv, *, tb):
    bsz, tlen, d = q.shape
    m = mk.shape[1]
    return pl.pallas_call(
        _xattn_body,
        grid=(bsz, tlen // tb),
        in_specs=[pl.BlockSpec((1, tb, d), lambda b, t: (b, t, 0)),
                  pl.BlockSpec((1, m, d), lambda b, t: (b, 0, 0)),
                  pl.BlockSpec((1, m, d), lambda b, t: (b, 0, 0))],
        out_specs=pl.BlockSpec((1, tb, d), lambda b, t: (b, t, 0)),
        out_shape=jax.ShapeDtypeStruct((bsz, tlen, d), BF16),
        compiler_params=_cparams(("parallel", "parallel"), 32),
    )(q, mk, mv)


def _matmul_residual_body(x_ref, a_ref, w_ref, o_ref):
    o_ref[...] = x_ref[...] + _dot(a_ref[...], w_ref[...])


def _matmul_residual(x, a, w, *, tm):
    n, d = x.shape
    return pl.pallas_call(
        _matmul_residual_body,
        grid=(n // tm,),
        in_specs=[pl.BlockSpec((tm, d), lambda i: (i, 0)),
                  pl.BlockSpec((tm, a.shape[1]), lambda i: (i, 0)),
                  pl.BlockSpec(w.shape, lambda i: (0, 0))],
        out_specs=pl.BlockSpec((tm, d), lambda i: (i, 0)),
        out_shape=jax.ShapeDtypeStruct((n, d), F32),
        compiler_params=_cparams(("parallel",), 40),
    )(x, a, w)


def _ffn_body(x_ref, g_ref, gate_ref, w1_ref, w3_ref, w2_ref, o_ref, h_ref, acc_ref, *, gated):
    e = pl.program_id(1)
    j = pl.program_id(2)

    @pl.when((e == 0) & (j == 0))
    def _():
        h_ref[...] = (_rms_rows(x_ref[...]) * g_ref[...]).astype(BF16)
        acc_ref[...] = jnp.zeros_like(acc_ref)

    h = h_ref[...]
    z = (_silu(_dot(h, w1_ref[0])) * _dot(h, w3_ref[0])).astype(BF16)
    y = _dot(z, w2_ref[0])
    if gated:
        gt = gate_ref[...]
        lane = lax.broadcasted_iota(jnp.int32, gt.shape, 1)
        y = y * jnp.sum(jnp.where(lane == e, gt, 0.0), axis=-1, keepdims=True)
    acc_ref[...] += y

    @pl.when((e == pl.num_programs(1) - 1) & (j == pl.num_programs(2) - 1))
    def _():
        o_ref[...] = x_ref[...] + acc_ref[...]


def _ffn(x, g, gates, w1, w3, w2, *, tm, tf, gated):
    n, d = x.shape
    ne, _, f = w1.shape
    return pl.pallas_call(
        functools.partial(_ffn_body, gated=gated),
        grid=(n // tm, ne, f // tf),
        in_specs=[pl.BlockSpec((tm, d), lambda i, e, j: (i, 0)),
                  pl.BlockSpec((1, d), lambda i, e, j: (0, 0)),
                  pl.BlockSpec((tm, gates.shape[1]), lambda i, e, j: (i, 0)),
                  pl.BlockSpec((1, d, tf), lambda i, e, j: (e, 0, j)),
                  pl.BlockSpec((1, d, tf), lambda i, e, j: (e, 0, j)),
                  pl.BlockSpec((1, tf, d), lambda i, e, j: (e, j, 0))],
        out_specs=pl.BlockSpec((tm, d), lambda i, e, j: (i, 0)),
        out_shape=jax.ShapeDtypeStruct((n, d), F32),
        scratch_shapes=[pltpu.VMEM((tm, d), BF16), pltpu.VMEM((tm, d), F32)],
        compiler_params=_cparams(("parallel", "arbitrary", "arbitrary"), 56),
    )(x, g.reshape(1, d), gates, w1, w3, w2)


def _router_body(x_ref, g_ref, rt_ref, o_ref):
    h = _rms_rows(x_ref[...]) * g_ref[...]
    logit = [jnp.sum(h * rt_ref[e:e + 1, :], axis=-1, keepdims=True) for e in range(N_EXPERTS)]
    m = functools.reduce(jnp.maximum, logit)
    ex = [jnp.exp(l - m) for l in logit]
    z = functools.reduce(lambda a, b: a + b, ex)
    p = [e / z for e in ex]
    p1, i1 = p[0], jnp.zeros_like(p[0], dtype=jnp.int32)
    for e in range(1, N_EXPERTS):
        better = p[e] > p1
        p1 = jnp.where(better, p[e], p1)
        i1 = jnp.where(better, e, i1)
    p2, i2 = jnp.full_like(p1, -1.0), jnp.zeros_like(i1)
    for e in range(N_EXPERTS):
        better = (p[e] > p2) & (i1 != e)
        p2 = jnp.where(better, p[e], p2)
        i2 = jnp.where(better, e, i2)
    tot = p1 + p2
    lane = lax.broadcasted_iota(jnp.int32, o_ref.shape, 1)
    o_ref[...] = jnp.where(lane == i1, p1 / tot, 0.0) + jnp.where(lane == i2, p2 / tot, 0.0)


def _router(x, g, router_t, *, tm):
    n, d = x.shape
    return pl.pallas_call(
        _router_body,
        grid=(n // tm,),
        in_specs=[pl.BlockSpec((tm, d), lambda i: (i, 0)),
                  pl.BlockSpec((1, d), lambda i: (0, 0)),
                  pl.BlockSpec(router_t.shape, lambda i: (0, 0))],
        out_specs=pl.BlockSpec((tm, N_EXPERTS), lambda i: (i, 0)),
        out_shape=jax.ShapeDtypeStruct((n, N_EXPERTS), F32),
        compiler_params=_cparams(("parallel",), 32),
    )(x, g.reshape(1, d), router_t)


def _final_norm_body(x_ref, g_ref, o_ref):
    o_ref[...] = _rms_rows(x_ref[...]) * g_ref[...]


def _final_norm(x, g, *, tm):
    n, d = x.shape
    return pl.pallas_call(
        _final_norm_body,
        grid=(n // tm,),
        in_specs=[pl.BlockSpec((tm, d), lambda i: (i, 0)), pl.BlockSpec((1, d), lambda i: (0, 0))],
        out_specs=pl.BlockSpec((tm, d), lambda i: (i, 0)),
        out_shape=jax.ShapeDtypeStruct((n, d), F32),
        compiler_params=_cparams(("parallel",), 32),
    )(x, g.reshape(1, d))


def _layer_weights(l, p):
    w = p['w_in'][l]
    w_main = jnp.concatenate([w[:, :_O_GA], w[:, _O_SUV:_O_RV], w[:, _O_MG:], w[:, _O_RV:_O_MG]],
                             axis=1).astype(BF16)
    w_ga = jnp.pad(w[:, _O_GA:_O_SUV], ((0, 0), (0, LANES - GLA_RANK))).astype(BF16)
    wa2 = p['gla_wa2'][l].reshape(GLA_RANK, GLA_HEADS, GLA_DK).transpose(1, 0, 2)
    wa2 = jnp.pad(wa2, ((0, 0), (0, LANES - GLA_RANK), (0, 0))).astype(BF16)
    return dict(
        w_main=w_main, w_ga=w_ga, wa2=wa2,
        ba=p['gla_ba'][l].reshape(GLA_HEADS, 1, GLA_DK),
        gnorm=p['gla_norm'][l].reshape(1, GLA_DV),
        pa=p['proj_a'][l].astype(BF16), pb=p['proj_b'][l].astype(BF16), pc=p['proj_c'][l].astype(BF16),
        wo=p['w_out'][l].astype(BF16), wq=p['x_wq'][l].astype(BF16), xwo=p['x_wo'][l].astype(BF16))


def _group_layer(x, l, lw, p, s_gla, s_ret, mem_k, mem_v, pos, *, bsz, tlen, tm, prompt):
    cols, ga = _in_proj(x, p['norm_mix'][l], lw['w_main'], lw['w_ga'], tm=tm, tn=2048)
    cols_flat = cols
    cols = cols.reshape(bsz, tlen, PROJ_W)
    ga = ga.reshape(bsz, tlen, LANES)
    if prompt:
        tpad, t_valid = tlen, None
        sgu_ws, sgu_bs = p['sgu_ws'][l], p['sgu_bs'][l]
        gla_kw = dict(tb=512, chunk=GLA_CHUNK, t_valid=GLA_CHUNK)
        ret_kw = dict(tb=512, chunk=RET_CHUNK, t_valid=RET_CHUNK)
        sgu_kw = dict(tb=512, chunk=SGU_CHUNK, emit_vn=False)
        xat_tb = 512
    else:
        tpad = SAMPLE_PAD_T
        padt = ((0, 0), (0, tpad - tlen), (0, 0))
        cols = jnp.pad(cols, padt)
        ga = jnp.pad(ga, padt)
        pos = jnp.pad(pos, (0, tpad - tlen))
        sgu_ws = jnp.pad(p['sgu_ws'][l][:, :tlen, :tlen], ((0, 0), (0, tpad - tlen), (0, tpad - tlen)))
        sgu_bs = jnp.pad(p['sgu_bs'][l][:, :tlen], ((0, 0), (0, tpad - tlen)))
        gla_kw = dict(tb=tpad, chunk=tpad, t_valid=tlen)
        ret_kw = dict(tb=tpad, chunk=tpad, t_valid=tlen)
        sgu_kw = dict(tb=tpad, chunk=tpad, emit_vn=True)
        xat_tb = tpad

    o_a, s_gla_new = _gla(cols, ga, lw['wa2'], lw['ba'], lw['gnorm'], s_gla, **gla_kw)
    o_c, s_ret_new = _ret(cols, pos, s_ret, **ret_kw)
    sgu_out = _sgu(cols, sgu_ws, sgu_bs, p['sgu_ln_g'][l], p['sgu_ln_b'][l], **sgu_kw)
    o_b = sgu_out[0]
    vn = None if prompt else sgu_out[1][:, :tlen]

    if not prompt:
        o_a, o_b, o_c = (a[:, :tlen] for a in (o_a, o_b, o_c))
    n = bsz * tlen
    x = _merge(x, o_a.reshape(n, -1), o_b.reshape(n, -1), o_c.reshape(n, -1), cols_flat,
               lw['pa'], lw['pb'], lw['pc'], lw['wo'], tm=tm // 4)

    q = _norm_matmul(x, p['norm_x'][l], lw['wq'], tm=tm // 2, tn=D_MODEL, out_dtype=BF16)
    q = q.reshape(bsz, tlen, D_MODEL)
    if not prompt:
        q = jnp.pad(q, ((0, 0), (0, tpad - tlen), (0, 0)))
    att = _xattn(q, mem_k.reshape(bsz, -1, D_MODEL), mem_v.reshape(bsz, -1, D_MODEL), tb=xat_tb)
    if not prompt:
        att = att[:, :tlen]
    x = _matmul_residual(x, att.reshape(n, D_MODEL), lw['xwo'], tm=tm // 2)
    return x, s_gla_new, s_ret_new, vn


def kernel(x_prompt, x_sample, state_gla, state_ret, cache_mem_k, cache_mem_v, mem_prompt, norm_mix, norm_x, norm_mem, norm_ffn, norm_final, w_in, gla_wa2, gla_ba, gla_norm, sgu_ln_g, sgu_ln_b, sgu_ws, sgu_bs, proj_a, proj_b, proj_c, w_out, x_wq, x_wk, x_wv, x_wo, ffn_w1, ffn_w3, ffn_w2, moe_router, moe_w1, moe_w3, moe_w2):
    p = dict(norm_mix=norm_mix, norm_x=norm_x, norm_ffn=norm_ffn, w_in=w_in, gla_wa2=gla_wa2, gla_ba=gla_ba,
             gla_norm=gla_norm, sgu_ln_g=sgu_ln_g, sgu_ln_b=sgu_ln_b, sgu_ws=sgu_ws, sgu_bs=sgu_bs,
             proj_a=proj_a, proj_b=proj_b, proj_c=proj_c, w_out=w_out, x_wq=x_wq, x_wo=x_wo)
    bp, tp, d = x_prompt.shape
    bs, ts, _ = x_sample.shape
    n_mem = mem_prompt.shape[1]

    mem2d = mem_prompt.reshape(bp * n_mem, d)
    p_mem_k, p_mem_v = [], []
    for l in range(DEPTH):
        wkv = jnp.concatenate([x_wk[l], x_wv[l]], axis=1).astype(BF16)
        kv = _norm_matmul(mem2d, norm_mem[l], wkv, tm=512, tn=2 * d, out_dtype=F32)
        p_mem_k.append(kv[:, :d].reshape(bp, n_mem, X_HEADS, X_HEAD_DIM))
        p_mem_v.append(kv[:, d:].reshape(bp, n_mem, X_HEADS, X_HEAD_DIM))
    p_mem_k = jnp.stack(p_mem_k)
    p_mem_v = jnp.stack(p_mem_v)

    pos_p = jnp.arange(tp, dtype=jnp.int32)
    pos_s = PAST_LEN + jnp.arange(ts, dtype=jnp.int32)
    xp = x_prompt.reshape(bp * tp, d)
    xs = x_sample.reshape(bs * ts, d)
    zeros_gla = jnp.zeros((bp, GLA_HEADS, GLA_DK, GLA_DV), F32)
    zeros_ret = jnp.zeros((bp, RET_HEADS, RET_DK, RET_DV), F32)
    tm_p, tm_s = 2048, bs * ts

    pg, pr, sg, sr, sv = [], [], [], [], []
    for l in range(DEPTH):
        lw = _layer_weights(l, p)
        xp, g1, r1, _ = _group_layer(xp, l, lw, p, zeros_gla, zeros_ret, p_mem_k[l], p_mem_v[l], pos_p,
                                     bsz=bp, tlen=tp, tm=tm_p, prompt=True)
        xs, g2, r2, v2 = _group_layer(xs, l, lw, p, state_gla[l], state_ret[l], cache_mem_k[l], cache_mem_v[l],
                                      pos_s, bsz=bs, tlen=ts, tm=tm_s, prompt=False)
        if l % 2 == 0:
            w1 = ffn_w1[l // 2].astype(BF16)[None]
            w3 = ffn_w3[l // 2].astype(BF16)[None]
            w2 = ffn_w2[l // 2].astype(BF16)[None]
            ones_p = jnp.ones((xp.shape[0], 1), F32)
            ones_s = jnp.ones((xs.shape[0], 1), F32)
            xp = _ffn(xp, norm_ffn[l], ones_p, w1, w3, w2, tm=512, tf=1408, gated=False)
            xs = _ffn(xs, norm_ffn[l], ones_s, w1, w3, w2, tm=512, tf=1408, gated=False)
        else:
            w1 = moe_w1[l // 2].astype(BF16)
            w3 = moe_w3[l // 2].astype(BF16)
            w2 = moe_w2[l // 2].astype(BF16)
            rt = moe_router[l // 2].T
            gates_p = _router(xp, norm_ffn[l], rt, tm=1024)
            gates_s = _router(xs, norm_ffn[l], rt, tm=512)
            xp = _ffn(xp, norm_ffn[l], gates_p, w1, w3, w2, tm=512, tf=1408, gated=True)
            xs = _ffn(xs, norm_ffn[l], gates_s, w1, w3, w2, tm=512, tf=1408, gated=True)
        pg.append(g1); pr.append(r1); sg.append(g2); sr.append(r2); sv.append(v2)

    y_prompt = _final_norm(xp, norm_final, tm=2048).reshape(bp, tp, d)
    y_sample = _final_norm(xs, norm_final, tm=512).reshape(bs, ts, d)
    return (y_prompt, y_sample, jnp.stack(pg), jnp.stack(pr), p_mem_k, p_mem_v,
            jnp.stack(sg), jnp.stack(sr), jnp.stack(sv))
```
